```python
import jax, jax.numpy as jnp
from jax import lax
import numpy as np

D_MODEL = 2048
BATCH = 32
SEQ = 256
DEPTH = 2
DEC_BATCH = 2
DEC_SEQ = 1024
PAST_LEN = 512

F32 = jnp.float32
EPS = 1e-6
GRID_W = 64
N_MIXERS = 2
N_ATTN_LAYERS = (DEPTH + 1) // 2
N_GLA_LAYERS = DEPTH // 2
HEAD_DIM = 128
N_HEADS = D_MODEL // HEAD_DIM
N_KV_HEADS = N_HEADS // 4
ATTN_QKV_DIM = (N_HEADS + 2 * N_KV_HEADS) * HEAD_DIM
Q_BLOCK = 128
ROPE_BASE = 10000.0
GLA_HEADS = 4
GLA_DK = D_MODEL // 2 // GLA_HEADS
GLA_DV = D_MODEL // GLA_HEADS
GLA_HK = GLA_HEADS * GLA_DK
GLA_HV = GLA_HEADS * GLA_DV
GLA_GATE_RANK = 16
GLA_GATE_NORMALIZER = 16.0
GLA_CHUNK = 64
GLA_IN_DIM = 2 * GLA_HK + 2 * GLA_HV + 2 * GLA_GATE_RANK
N_EXPERTS = 16
N_GROUPS = 4
EXPERTS_PER_GROUP = N_EXPERTS // N_GROUPS
TOP_K = 2
D_EXPERT = 512

kernel_name = "hybrid_dit_gqa_gla_groupmoe_step"


def rmsnorm(x, gain):
    xf = x.astype(F32)
    y = xf * lax.rsqrt(jnp.mean(xf * xf, axis=-1, keepdims=True) + EPS)
    return (y * gain.astype(F32)).astype(x.dtype)


def ada_params(cond, w_ada, b_ada):
    m = jax.nn.silu(cond) @ w_ada + b_ada
    return jnp.split(m[:, None, :], 6, axis=-1)


def modulate(x, gain, shift, scale):
    return rmsnorm(x, gain) * (1.0 + scale) + shift


def axial_rope(x):
    n_tok = x.shape[1]
    rows = n_tok // GRID_W
    row = jnp.repeat(jnp.arange(rows, dtype=F32), GRID_W)
    col = jnp.tile(jnp.arange(GRID_W, dtype=F32), rows)
    n_pairs_axis = HEAD_DIM // 4
    inv = ROPE_BASE ** (-jnp.arange(n_pairs_axis, dtype=F32) / n_pairs_axis)
    ang = jnp.concatenate([row[:, None] * inv, col[:, None] * inv], axis=-1)
    cos = jnp.cos(ang)[None, :, None, :]
    sin = jnp.sin(ang)[None, :, None, :]
    xp = x.astype(F32).reshape(*x.shape[:-1], HEAD_DIM // 2, 2)
    x0, x1 = xp[..., 0], xp[..., 1]
    out = jnp.stack([x0 * cos - x1 * sin, x0 * sin + x1 * cos], axis=-1)
    return out.reshape(x.shape).astype(x.dtype)


def attn_project(h, w_qkv, q_gain, k_gain):
    B, T, _ = h.shape
    qkv = h @ w_qkv
    nq, nk = N_HEADS * HEAD_DIM, N_KV_HEADS * HEAD_DIM
    q = qkv[..., :nq].reshape(B, T, N_HEADS, HEAD_DIM)
    k = qkv[..., nq:nq + nk].reshape(B, T, N_KV_HEADS, HEAD_DIM)
    v = qkv[..., nq + nk:].reshape(B, T, N_KV_HEADS, HEAD_DIM)
    return rmsnorm(q, q_gain), rmsnorm(k, k_gain), v


def block_attention(q, k, v):
    B, T, H, hd = q.shape
    KV = k.shape[2]
    G = H // KV
    nb = T // Q_BLOCK
    qb = q.reshape(B, nb, Q_BLOCK, KV, G, hd).transpose(1, 0, 2, 3, 4, 5)
    kf, vf = k.astype(F32), v.astype(F32)
    scale = hd ** -0.5

    def one_block(qblk):
        s = jnp.einsum('bqkgd,bskd->bkgqs', qblk.astype(F32), kf) * scale
        p = jax.nn.softmax(s, axis=-1)
        return jnp.einsum('bkgqs,bskd->bqkgd', p, vf).astype(q.dtype)

    o = lax.map(one_block, qb)
    return o.transpose(1, 0, 2, 3, 4, 5).reshape(B, T, H * hd)


def gla_project(h, w_in, w_gate_up, b_gate):
    B, T, _ = h.shape
    proj = h @ w_in
    q, k, v, r, lr = jnp.split(proj, [GLA_HK, 2 * GLA_HK, 2 * GLA_HK + GLA_HV, 2 * GLA_HK + 2 * GLA_HV], axis=-1)
    q = q.reshape(B, T, GLA_HEADS, GLA_DK) * (GLA_DK ** -0.5)
    k = k.reshape(B, T, GLA_HEADS, GLA_DK)
    v = v.reshape(B, T, GLA_HEADS, GLA_DV)
    lr = lr.reshape(B, T, 2, GLA_GATE_RANK)
    z = jnp.einsum('btdr,drk->btdk', lr, w_gate_up) + b_gate
    log_a = jax.nn.log_sigmoid(z.astype(F32)) / GLA_GATE_NORMALIZER
    return q, k, v, r, log_a.reshape(B, T, 2, GLA_HEADS, GLA_DK)


def gla_chunked(q, k, v, log_a, s0):
    B, T, H, DK = q.shape
    DV = v.shape[-1]
    n = T // GLA_CHUNK

    def to_chunks(t):
        return t.astype(F32).reshape(B, n, GLA_CHUNK, H, t.shape[-1]).transpose(1, 0, 3, 2, 4)

    qc, kc, vc, ac = to_chunks(q), to_chunks(k), to_chunks(v), to_chunks(log_a)
    b = jnp.cumsum(ac, axis=3)
    b_end = b[:, :, :, -1, :]
    q_dec = qc * jnp.exp(b)
    k_inv = kc * jnp.exp(-b)
    k_end = kc * jnp.exp(b_end[:, :, :, None, :] - b)
    causal = jnp.tril(jnp.ones((GLA_CHUNK, GLA_CHUNK), F32))
    att = jnp.einsum('nbhik,nbhjk->nbhij', q_dec, k_inv) * causal
    o_intra = jnp.einsum('nbhij,nbhjv->nbhiv', att, vc)

    def step(state, xs):
        q_d, k_e, v_c, g_end = xs
        o_inter = jnp.einsum('bhik,bhkv->bhiv', q_d, state)
        state = state * jnp.exp(g_end)[..., None] + jnp.einsum('bhjk,bhjv->bhkv', k_e, v_c)
        return state, o_inter

    s_final, o_inter = lax.scan(step, s0.astype(F32), (q_dec, k_end, vc, b_end))
    o = (o_intra + o_inter).transpose(1, 0, 3, 2, 4).reshape(B, T, H, DV)
    return o, s_final


def gla_bidirectional(q, k, v, log_a, s0):
    o_f, s_f = gla_chunked(q, k, v, log_a[:, :, 0], s0[:, 0])
    flip = lambda t: jnp.flip(t, axis=1)
    o_b, s_b = gla_chunked(flip(q), flip(k), flip(v), flip(log_a[:, :, 1]), s0[:, 1])
    return o_f + flip(o_b), jnp.stack([s_f, s_b], axis=1)


def gla_output(o, r, h_dtype, o_gain, w_o):
    B, T = o.shape[:2]
    o = rmsnorm(o, o_gain).reshape(B, T, GLA_HV) * jax.nn.silu(r.astype(F32))
    return o.astype(h_dtype) @ w_o


def moe_ffn(x, w_router, router_bias, w_gate, w_up, w_down):
    B, T, D = x.shape
    xt = x.reshape(-1, D)
    scores = jax.nn.sigmoid((xt @ w_router).astype(F32))
    biased = scores + router_bias.astype(F32)
    grouped = biased.reshape(-1, N_GROUPS, EXPERTS_PER_GROUP)
    group_score = lax.top_k(grouped, TOP_K)[0].sum(-1)
    g_sel = jnp.argmax(group_score, axis=-1)
    in_group = jnp.take_along_axis(grouped, g_sel[:, None, None], axis=1)[:, 0]
    _, local = lax.top_k(in_group, TOP_K)
    idx = g_sel[:, None] * EXPERTS_PER_GROUP + local
    w = jnp.take_along_axis(scores, idx, axis=-1)
    w = w / jnp.sum(w, axis=-1, keepdims=True)
    gates = jnp.sum(jax.nn.one_hot(idx, N_EXPERTS, dtype=F32) * w[..., None], axis=1)

    def expert(acc, ew):
        wg, wu, wd, g = ew
        hdn = jax.nn.silu(xt @ wg) * (xt @ wu)
        return acc + g[:, None] * (hdn @ wd).astype(F32), None

    acc, _ = lax.scan(expert, jnp.zeros(xt.shape, F32), (w_gate, w_up, w_down, gates.T))
    return acc.astype(x.dtype).reshape(B, T, D)


def setup_inputs(seed: int = 0) -> dict:
    key = jax.random.key(seed)
    ks = iter(jax.random.split(key, 32))
    nrm = lambda shape, s=1.0: s * jax.random.normal(next(ks), shape, F32)
    D = D_MODEL
    return {
        "x_prompt": nrm((BATCH, SEQ, D)),
        "x_sample": nrm((DEC_BATCH, DEC_SEQ, D)),
        "cache_k": nrm((DEC_BATCH, N_ATTN_LAYERS, PAST_LEN, N_KV_HEADS, HEAD_DIM)),
        "cache_v": nrm((DEC_BATCH, N_ATTN_LAYERS, PAST_LEN, N_KV_HEADS, HEAD_DIM)),
        "state_gla": nrm((DEC_BATCH, N_GLA_LAYERS, 2, GLA_HEADS, GLA_DK, GLA_DV), GLA_DK ** -0.5),
        "c": nrm((DEC_BATCH, D)),
        "c_ctx": nrm((D,)),
        "w_ada": nrm((DEPTH, D, 6 * D), 0.5 * D ** -0.5),
        "b_ada": nrm((DEPTH, 6 * D), 0.02),
        "norm_mix": 1.0 + nrm((DEPTH, D), 0.1),
        "norm_ffn": 1.0 + nrm((DEPTH, D), 0.1),
        "attn_w_qkv": nrm((N_ATTN_LAYERS, D, ATTN_QKV_DIM), D ** -0.5),
        "attn_q_norm": 1.0 + nrm((N_ATTN_LAYERS, HEAD_DIM), 0.1),
        "attn_k_norm": 1.0 + nrm((N_ATTN_LAYERS, HEAD_DIM), 0.1),
        "attn_w_o": nrm((N_ATTN_LAYERS, N_HEADS * HEAD_DIM, D), (N_HEADS * HEAD_DIM) ** -0.5),
        "gla_w_in": nrm((N_GLA_LAYERS, D, GLA_IN_DIM), D ** -0.5),
        "gla_w_gate_up": nrm((N_GLA_LAYERS, 2, GLA_GATE_RANK, GLA_HK), GLA_GATE_RANK ** -0.5),
        "gla_b_gate": nrm((N_GLA_LAYERS, 2, GLA_HK), 0.5),
        "gla_o_norm": 1.0 + nrm((N_GLA_LAYERS, GLA_DV), 0.1),
        "gla_w_o": nrm((N_GLA_LAYERS, GLA_HV, D), GLA_HV ** -0.5),
        "moe_w_router": nrm((D, N_EXPERTS), D ** -0.5),
        "moe_router_bias": nrm((N_EXPERTS,), 0.01),
        "moe_w_gate": nrm((DEPTH, N_EXPERTS, D, D_EXPERT), D ** -0.5),
        "moe_w_up": nrm((DEPTH, N_EXPERTS, D, D_EXPERT), D ** -0.5),
        "moe_w_down": nrm((DEPTH, N_EXPERTS, D_EXPERT, D), D_EXPERT ** -0.5),
    }


def reference(x_prompt, x_sample, cache_k, cache_v, state_gla, c, c_ctx,
              w_ada, b_ada, norm_mix, norm_ffn,
              attn_w_qkv, attn_q_norm, attn_k_norm, attn_w_o,
              gla_w_in, gla_w_gate_up, gla_b_gate, gla_o_norm, gla_w_o,
              moe_w_router, moe_router_bias, moe_w_gate, moe_w_up, moe_w_down):
    xp, xs = x_prompt, x_sample
    ctx_cond = c_ctx[None, :]
    new_k, new_v, new_s = [], [], []
    for i in range(DEPTH):
        sh1p, sc1p, g1p, sh2p, sc2p, g2p = ada_params(ctx_cond, w_ada[i], b_ada[i])
        sh1s, sc1s, g1s, sh2s, sc2s, g2s = ada_params(c, w_ada[i], b_ada[i])
        hp = modulate(xp, norm_mix[i], sh1p, sc1p)
        hs = modulate(xs, norm_mix[i], sh1s, sc1s)
        j = i // N_MIXERS
        if i % N_MIXERS == 0:
            q, k, v = attn_project(hp, attn_w_qkv[j], attn_q_norm[j], attn_k_norm[j])
            mp = block_attention(q, k, v) @ attn_w_o[j]
            new_k.append(k)
            new_v.append(v)
            q, k, v = attn_project(hs, attn_w_qkv[j], attn_q_norm[j], attn_k_norm[j])
            q, k = axial_rope(q), axial_rope(k)
            k_all = jnp.concatenate([cache_k[:, j], k], axis=1)
            v_all = jnp.concatenate([cache_v[:, j], v], axis=1)
            ms = block_attention(q, k_all, v_all) @ attn_w_o[j]
        else:
            q, k, v, r, la = gla_project(hp, gla_w_in[j], gla_w_gate_up[j], gla_b_gate[j])
            s0 = jnp.zeros((hp.shape[0], 2, GLA_HEADS, GLA_DK, GLA_DV), F32)
            o, s_ctx = gla_bidirectional(q, k, v, la, s0)
            mp = gla_output(o, r, hp.dtype, gla_o_norm[j], gla_w_o[j])
            new_s.append(s_ctx)
            q, k, v, r, la = gla_project(hs, gla_w_in[j], gla_w_gate_up[j], gla_b_gate[j])
            o, _ = gla_bidirectional(q, k, v, la, state_gla[:, j])
            ms = gla_output(o, r, hs.dtype, gla_o_norm[j], gla_w_o[j])
        xp = xp + g1p * mp
        xs = xs + g1s * ms
        xp = xp + g2p * moe_ffn(modulate(xp, norm_ffn[i], sh2p, sc2p), moe_w_router, moe_router_bias,
                                moe_w_gate[i], moe_w_up[i], moe_w_down[i])
        xs = xs + g2s * moe_ffn(modulate(xs, norm_ffn[i], sh2s, sc2s), moe_w_router, moe_router_bias,
                                moe_w_gate[i], moe_w_up[i], moe_w_down[i])
    new_cache_k = jnp.stack(new_k, axis=1)
    new_cache_v = jnp.stack(new_v, axis=1)
    new_state_gla = jnp.stack(new_s, axis=1).astype(state_gla.dtype)
    return (xp, xs, new_cache_k, new_cache_v, new_state_gla)
```

```python
import functools

import jax
import jax.numpy as jnp
from jax import lax
from jax.experimental import pallas as pl
from jax.experimental.pallas import tpu as pltpu

F32 = jnp.float32
BF16 = jnp.bfloat16
I32 = jnp.int32

D_MODEL = 2048
EPS = 1e-6
GRID_W = 64
HEAD_DIM = 128
N_HEADS = 16
N_KV_HEADS = 4
ROPE_BASE = 10000.0
GLA_HEADS = 4
GLA_DK = 256
GLA_DV = 512
GLA_HK = GLA_HEADS * GLA_DK
GLA_HV = GLA_HEADS * GLA_DV
GLA_GATE_RANK = 16
GLA_GATE_NORMALIZER = 16.0
GLA_CHUNK = 64
N_EXPERTS = 16
N_GROUPS = 4
EXPERTS_PER_GROUP = 4
D_EXPERT = 512

PAIR_A = (0, 0, 1, 1, 0, 2)
PAIR_B = (1, 2, 2, 3, 3, 3)
N_PAIRS = len(PAIR_A)
N_BUCKETS = N_GROUPS * N_PAIRS
BUCKET_ROWS = 32
LANE = 128
ROW_W = D_MODEL + LANE
VMEM_LIMIT = 56 * 1024 * 1024

TM = 256
TMOE = 256
ADA_TN = 1536


def _cparams(*sem):
    return pltpu.CompilerParams(dimension_semantics=sem, vmem_limit_bytes=VMEM_LIMIT)


def _resident(block_shape, index_map):
    return pl.BlockSpec(block_shape, index_map, pipeline_mode=pl.Buffered(1))


def _dot(a, b):
    return jnp.dot(a, b, preferred_element_type=F32)


def _dot_nt(a, b):
    return lax.dot_general(a, b, (((1,), (1,)), ((), ())), preferred_element_type=F32)


def _split_bf16(x):
    hi = x.astype(BF16)
    lo = (x - hi.astype(F32)).astype(BF16)
    return hi, lo


def _modulate(x, gain, shift, scale):
    ms = jnp.mean(x * x, axis=-1, keepdims=True)
    y = (x * lax.rsqrt(ms + EPS)) * gain
    return y * (1.0 + scale) + shift


def _rms(x, gain):
    ms = jnp.mean(x * x, axis=-1, keepdims=True)
    return (x * lax.rsqrt(ms + EPS)) * gain


def _ada_kernel(c_ref, w_ref, b_ref, o_ref):
    c = c_ref[...]
    s = (c * jax.nn.sigmoid(c)).astype(BF16)
    o_ref[0] = _dot(s, w_ref[0].astype(BF16)) + b_ref[0]


def _ada_call(cond8, w_ada, b_ada):
    depth, d, n6 = w_ada.shape
    return pl.pallas_call(
        _ada_kernel,
        grid=(depth, n6 // ADA_TN),
        in_specs=[
            pl.BlockSpec((8, d), lambda l, j: (0, 0)),
            pl.BlockSpec((1, d, ADA_TN), lambda l, j: (l, 0, j)),
            pl.BlockSpec((1, 1, ADA_TN), lambda l, j: (l, 0, j)),
        ],
        out_specs=pl.BlockSpec((1, 8, ADA_TN), lambda l, j: (l, 0, j)),
        out_shape=jax.ShapeDtypeStruct((depth, 8, n6), F32),
        compiler_params=_cparams("parallel", "parallel"),
        name="ada",
    )(cond8, w_ada, b_ada.reshape(depth, 1, n6))


class _Stream:
    def __init__(self, batch, seq, cond_base):
        self.batch, self.seq, self.rows, self.cond_base = batch, seq, batch * seq, cond_base

    def cond_map(self, tm):
        tiles_per_seq = self.seq // tm
        if self.cond_base == 0:
            return lambda i, *_: (0, 0, 0)
        return lambda i, *_: (self.cond_base + i // tiles_per_seq, 0, 0)


def _mod_spec(stream, tm):
    return pl.BlockSpec((1, 1, D_MODEL), stream.cond_map(tm))


def _rope(xh, cos, sin):
    lane = lax.broadcasted_iota(I32, xh.shape, 1)
    nxt = pltpu.roll(xh, HEAD_DIM - 1, 1)
    prv = pltpu.roll(xh, 1, 1)
    return xh * cos + jnp.where(lane % 2 == 0, nxt, prv) * sin


def _qkv_kernel(x_ref, gain_ref, sh_ref, sc_ref, w_ref, qg_ref, kg_ref, *rest, rope):
    if rope:
        cos_ref, sin_ref, q_ref, k_ref, v_ref = rest
        cos, sin = cos_ref[...], sin_ref[...]
    else:
        q_ref, k_ref, v_ref = rest
    h = _modulate(x_ref[...], gain_ref[...], sh_ref[0], sc_ref[0]).astype(BF16)
    nq, nk = N_HEADS * HEAD_DIM, N_KV_HEADS * HEAD_DIM
    q = _dot(h, w_ref[:, :nq])
    for hh in range(N_HEADS):
        xh = _rms(q[:, hh * HEAD_DIM:(hh + 1) * HEAD_DIM], qg_ref[...])
        if rope:
            xh = _rope(xh, cos, sin)
        q_ref[:, hh * HEAD_DIM:(hh + 1) * HEAD_DIM] = xh.astype(BF16)
    kv = _dot(h, w_ref[:, nq:])
    for hh in range(N_KV_HEADS):
        xh = _rms(kv[:, hh * HEAD_DIM:(hh + 1) * HEAD_DIM], kg_ref[...])
        if rope:
            xh = _rope(xh, cos, sin)
        k_ref[:, hh * HEAD_DIM:(hh + 1) * HEAD_DIM] = xh
    v_ref[...] = kv[:, nk:]


def _qkv_call(stream, x, gain, sh, sc, w, qg, kg, rope_tabs):
    n = stream.rows
    nqkv = w.shape[1]
    nk = N_KV_HEADS * HEAD_DIM
    row = lambda i: (i, 0)
    fixed = lambda i: (0, 0)
    in_specs = [
        pl.BlockSpec((TM, D_MODEL), row),
        pl.BlockSpec((1, D_MODEL), fixed),
        _mod_spec(stream, TM), _mod_spec(stream, TM),
        _resident((D_MODEL, nqkv), fixed),
        pl.BlockSpec((1, HEAD_DIM), fixed),
        pl.BlockSpec((1, HEAD_DIM), fixed),
    ]
    args = [x, gain, sh, sc, w, qg, kg]
    if rope_tabs is not None:
        tiles_per_seq = stream.seq // TM
        pos = lambda i: (i % tiles_per_seq, 0)
        in_specs += [pl.BlockSpec((TM, HEAD_DIM), pos), pl.BlockSpec((TM, HEAD_DIM), pos)]
        args += list(rope_tabs)
    return pl.pallas_call(
        functools.partial(_qkv_kernel, rope=rope_tabs is not None),
        grid=(n // TM,),
        in_specs=in_specs,
        out_specs=[pl.BlockSpec((TM, N_HEADS * HEAD_DIM), row),
                   pl.BlockSpec((TM, nk), row), pl.BlockSpec((TM, nk), row)],
        out_shape=[jax.ShapeDtypeStruct((n, N_HEADS * HEAD_DIM), BF16),
                   jax.ShapeDtypeStruct((n, nk), F32), jax.ShapeDtypeStruct((n, nk), F32)],
        compiler_params=_cparams("parallel"),
        name="qkv_rope" if rope_tabs is not None else "qkv",
    )(*args)


def _rope_tables(seq):
    t = jnp.arange(seq)
    rowp = (t // GRID_W).astype(F32)
    colp = (t % GRID_W).astype(F32)
    n_pairs_axis = HEAD_DIM // 4
    inv = ROPE_BASE ** (-jnp.arange(n_pairs_axis, dtype=F32) / n_pairs_axis)
    ang = jnp.concatenate([rowp[:, None] * inv, colp[:, None] * inv], axis=-1)
    cos = jnp.repeat(jnp.cos(ang), 2, axis=-1)
    sin = jnp.sin(ang)
    sin = jnp.stack([-sin, sin], axis=-1).reshape(seq, HEAD_DIM)
    return cos, sin


def _attn_kernel(q_ref, *refs, nseg, scale):
    o_ref = refs[2 * nseg]
    group = N_HEADS // N_KV_HEADS
    for g in range(N_KV_HEADS):
        sl = slice(g * HEAD_DIM, (g + 1) * HEAD_DIM)
        ks = [refs[2 * s][0, :, sl].astype(BF16) for s in range(nseg)]
        vs = [refs[2 * s + 1][0, :, sl].astype(BF16) for s in range(nseg)]
        for j in range(group):
            hs = slice((g * group + j) * HEAD_DIM, (g * group + j + 1) * HEAD_DIM)
            qh = q_ref[0, :, hs]
            ss = [_dot_nt(qh, k) * scale for k in ks]
            m = ss[0].max(axis=-1, keepdims=True)
            for s in ss[1:]:
                m = jnp.maximum(m, s.max(axis=-1, keepdims=True))
            es = [jnp.exp(s - m) for s in ss]
            den = es[0].sum(axis=-1, keepdims=True)
            for e in es[1:]:
                den = den + e.sum(axis=-1, keepdims=True)
            acc = _dot(es[0].astype(BF16), vs[0])
            for e, v in zip(es[1:], vs[1:]):
                acc = acc + _dot(e.astype(BF16), v)
            o_ref[0, :, hs] = (acc / den).astype(BF16)


def _attn_call(q, kvs, tq):
    b, t, dq = q.shape
    nseg = len(kvs)
    in_specs = [pl.BlockSpec((1, tq, dq), lambda bi, i: (bi, i, 0))]
    args = [q]
    for k, v in kvs:
        for a in (k, v):
            in_specs.append(pl.BlockSpec((1,) + a.shape[1:], lambda bi, i: (bi, 0, 0)))
            args.append(a)
    return pl.pallas_call(
        functools.partial(_attn_kernel, nseg=nseg, scale=HEAD_DIM ** -0.5),
        grid=(b, t // tq),
        in_specs=in_specs,
        out_specs=pl.BlockSpec((1, tq, dq), lambda bi, i: (bi, i, 0)),
        out_shape=jax.ShapeDtypeStruct((b, t, dq), BF16),
        compiler_params=_cparams("parallel", "parallel"),
        name="attn%d" % nseg,
    )(*args)


def _route_rows(sc, bi):
    scr = [sc[e:e + 1, :] for e in range(N_EXPERTS)]
    bir = [bi[e:e + 1, :] for e in range(N_EXPERTS)]
    gs = []
    for g in range(N_GROUPS):
        a, b, c, d = bir[4 * g:4 * g + 4]
        gs.append(jnp.maximum(jnp.maximum(jnp.maximum(a + b, a + c), jnp.maximum(a + d, b + c)),
                              jnp.maximum(b + d, c + d)))
    best, gsel = gs[0], jnp.zeros_like(gs[0], dtype=I32)
    for g in range(1, N_GROUPS):
        better = gs[g] > best
        gsel = jnp.where(better, g, gsel)
        best = jnp.where(better, gs[g], best)
    v, u = [], []
    for j in range(EXPERTS_PER_GROUP):
        vj, uj = bir[j], scr[j]
        for g in range(1, N_GROUPS):
            vj = jnp.where(gsel == g, bir[4 * g + j], vj)
            uj = jnp.where(gsel == g, scr[4 * g + j], uj)
        v.append(vj)
        u.append(uj)
    sel = []
    for i in range(EXPERTS_PER_GROUP):
        cnt = jnp.zeros_like(gsel)
        for j in range(EXPERTS_PER_GROUP):
            if j == i:
                continue
            beats = (v[j] > v[i]) | ((v[j] == v[i]) & (j < i)) if j < i else (v[j] > v[i])
            cnt = cnt + beats.astype(I32)
        sel.append(cnt < 2)
    wsum = jnp.zeros_like(u[0])
    for i in range(EXPERTS_PER_GROUP):
        wsum = wsum + jnp.where(sel[i], u[i], 0.0)
    gate = [u[i] / wsum for i in range(EXPERTS_PER_GROUP)]
    code = sum(jnp.where(sel[i], 1 << i, 0) for i in range(EXPERTS_PER_GROUP))
    pair = jnp.zeros_like(gsel)
    for p in range(N_PAIRS):
        pair = jnp.where(code == (1 << PAIR_A[p]) + (1 << PAIR_B[p]), p, pair)
    w_lo = jnp.where(sel[0], gate[0], jnp.where(sel[1], gate[1], gate[2]))
    w_hi = jnp.where(sel[3], gate[3], jnp.where(sel[2], gate[2], gate[1]))
    return gsel * N_PAIRS + pair, w_lo, w_hi


def _proj_route_kernel(a_ref, wo_ref, x_ref, g1_ref, gain_ref, sh_ref, sc_ref, wr_ref, rb_ref, cin_ref,
                       xn_ref, ri_ref, rw_ref, cout_ref, cnt_sc):
    i = pl.program_id(0)
    tm = x_ref.shape[0]

    @pl.when(i == 0)
    def _():
        cnt_sc[...] = cin_ref[...]

    xn = x_ref[...] + g1_ref[0] * _dot(a_ref[...], wo_ref[...])
    xn_ref[...] = xn
    h2 = _modulate(xn, gain_ref[...], sh_ref[0], sc_ref[0])
    hi, lo = _split_bf16(h2)
    p1 = _dot(hi, wr_ref[...])
    p2 = _dot(lo, wr_ref[...])
    logits = (p1 + pltpu.roll(p1, LANE - N_EXPERTS, 1) + p2).T[:N_EXPERTS]
    score = jax.nn.sigmoid(logits)
    bucket, w_lo, w_hi = _route_rows(score, score + rb_ref[...])

    onehot = lax.broadcasted_iota(I32, (BUCKET_ROWS, tm), 0) == bucket
    tri = (lax.broadcasted_iota(I32, (tm, tm), 0) <= lax.broadcasted_iota(I32, (tm, tm), 1))
    cum = _dot(onehot.astype(BF16), tri.astype(BF16))
    base = cnt_sc[:, 0:1]
    rank = jnp.sum(jnp.where(onehot, cum - 1.0 + base, 0.0), axis=0, keepdims=True)
    cnt_new = cnt_sc[...] + cum[:, tm - 1:tm]
    cnt_sc[...] = cnt_new
    cout_ref[...] = cnt_new

    ri_ref[0] = jnp.zeros(ri_ref.shape[1:], I32)
    ri_ref[0, 0:1, :] = bucket
    ri_ref[0, 1:2, :] = rank.astype(I32)
    rw_ref[0] = jnp.zeros(rw_ref.shape[1:], F32)
    rw_ref[0, 0:1, :] = w_lo
    rw_ref[0, 1:2, :] = w_hi


def _proj_route_call(stream, a, wo, x, g1, gain2, sh2, sc2, wr2, rb, cnt_in):
    n = stream.rows
    nt = n // TM
    row = lambda i: (i, 0)
    fixed = lambda i: (0, 0)
    tile3 = lambda i: (i, 0, 0)
    return pl.pallas_call(
        _proj_route_kernel,
        grid=(nt,),
        in_specs=[
            pl.BlockSpec((TM, a.shape[1]), row),
            _resident(wo.shape, fixed),
            pl.BlockSpec((TM, D_MODEL), row),
            _mod_spec(stream, TM),
            pl.BlockSpec((1, D_MODEL), fixed),
            _mod_spec(stream, TM), _mod_spec(stream, TM),
            _resident(wr2.shape, fixed),
            pl.BlockSpec(rb.shape, fixed),
            pl.BlockSpec(cnt_in.shape, fixed),
        ],
        out_specs=[
            pl.BlockSpec((TM, D_MODEL), row),
            pl.BlockSpec((1, 8, TM), tile3),
            pl.BlockSpec((1, 8, TM), tile3),
            pl.BlockSpec(cnt_in.shape, fixed),
        ],
        out_shape=[
            jax.ShapeDtypeStruct((n, D_MODEL), F32),
            jax.ShapeDtypeStruct((nt, 8, TM), I32),
            jax.ShapeDtypeStruct((nt, 8, TM), F32),
            jax.ShapeDtypeStruct(cnt_in.shape, F32),
        ],
        scratch_shapes=[pltpu.VMEM(cnt_in.shape, F32)],
        compiler_params=_cparams("arbitrary"),
        name="proj_route",
    )(a, wo, x, g1, gain2, sh2, sc2, wr2, rb, cnt_in)


def _row_copy_out(stage, slot, r, dst_hbm, d, sem):
    return pltpu.make_async_copy(stage.at[slot, pl.ds(r, 1), :], dst_hbm.at[pl.ds(d, 1), :], sem.at[slot])


def _dispatch_kernel(dest_ref, xn_ref, gain_ref, sh_ref, sc_ref, wc_ref, sorted_in, sorted_out, stage, sem):
    del sorted_in
    i = pl.program_id(0)
    n = pl.num_programs(0)
    tm = xn_ref.shape[0]
    slot = i % 2

    def wait_slot(s):
        pltpu.make_async_copy(stage.at[s], sorted_out.at[pl.ds(0, tm), :], sem.at[s]).wait()

    @pl.when(i >= 2)
    def _():
        wait_slot(slot)

    stage[slot, :, :D_MODEL] = _modulate(xn_ref[...], gain_ref[...], sh_ref[0], sc_ref[0])
    stage[slot, :, D_MODEL:] = wc_ref[...]

    def issue(r, carry):
        _row_copy_out(stage, slot, r, sorted_out, dest_ref[i * tm + r], sem).start()
        return carry

    lax.fori_loop(0, tm, issue, 0)

    @pl.when(i == n - 1)
    def _():
        wait_slot(slot)

        @pl.when(n >= 2)
        def _():
            wait_slot(1 - slot)


def _dispatch_call(stream, dest, xn, gain2, sh2, sc2, wcols, sorted_buf):
    n = stream.rows
    row = lambda i, d: (i, 0)
    fixed = lambda i, d: (0, 0)
    grid_spec = pltpu.PrefetchScalarGridSpec(
        num_scalar_prefetch=1,
        grid=(n // TM,),
        in_specs=[
            pl.BlockSpec((TM, D_MODEL), row),
            pl.BlockSpec((1, D_MODEL), fixed),
            _mod_spec(stream, TM), _mod_spec(stream, TM),
            pl.BlockSpec((TM, LANE), row),
            pl.BlockSpec(memory_space=pl.ANY),
        ],
        out_specs=pl.BlockSpec(memory_space=pl.ANY),
        scratch_shapes=[pltpu.VMEM((2, TM, ROW_W), F32), pltpu.SemaphoreType.DMA((2,))],
    )
    return pl.pallas_call(
        _dispatch_kernel,
        grid_spec=grid_spec,
        out_shape=jax.ShapeDtypeStruct(sorted_buf.shape, F32),
        input_output_aliases={6: 0},
        compiler_params=_cparams("arbitrary"),
        name="dispatch",
    )(dest, xn, gain2, sh2, sc2, wcols, sorted_buf)


def _moe_kernel(ea_ref, eb_ref, tidx_ref, nused_ref, xs_ref, wga, wua, wda, wgb, wub, wdb, y_ref):
    del ea_ref, eb_ref, tidx_ref
    t = pl.program_id(0)

    @pl.when(t < nused_ref[0])
    def _():
        x = xs_ref[:, :D_MODEL].astype(BF16)

        def expert(wg, wu, wd):
            g = _dot(x, wg[0])
            u = _dot(x, wu[0])
            hdn = (g * jax.nn.sigmoid(g)) * u
            return _dot(hdn.astype(BF16), wd[0])

        ya = expert(wga, wua, wda)
        yb = expert(wgb, wub, wdb)
        y_ref[...] = xs_ref[:, D_MODEL:D_MODEL + 1] * ya + xs_ref[:, D_MODEL + 1:D_MODEL + 2] * yb

    @pl.when(t >= nused_ref[0])
    def _():
        y_ref[...] = jnp.zeros(y_ref.shape, F32)


def _moe_call(ea, eb, tidx, nused, sorted_buf, wg, wu, wd):
    rows = sorted_buf.shape[0]
    xmap = lambda t, ea, eb, ti, nu: (ti[t], 0)
    amap = lambda t, ea, eb, ti, nu: (ea[t], 0, 0)
    bmap = lambda t, ea, eb, ti, nu: (eb[t], 0, 0)
    up_blk = (1, D_MODEL, D_EXPERT)
    dn_blk = (1, D_EXPERT, D_MODEL)
    grid_spec = pltpu.PrefetchScalarGridSpec(
        num_scalar_prefetch=4,
        grid=(rows // TMOE,),
        in_specs=[
            pl.BlockSpec((TMOE, ROW_W), xmap),
            pl.BlockSpec(up_blk, amap), pl.BlockSpec(up_blk, amap), pl.BlockSpec(dn_blk, amap),
            pl.BlockSpec(up_blk, bmap), pl.BlockSpec(up_blk, bmap), pl.BlockSpec(dn_blk, bmap),
        ],
        out_specs=pl.BlockSpec((TMOE, D_MODEL), lambda t, ea, eb, ti, nu: (t, 0)),
    )
    return pl.pallas_call(
        _moe_kernel,
        grid_spec=grid_spec,
        out_shape=jax.ShapeDtypeStruct((rows, D_MODEL), F32),
        compiler_params=_cparams("arbitrary"),
        name="moe",
    )(ea, eb, tidx, nused, sorted_buf, wg, wu, wd, wg, wu, wd)


def _combine_kernel(dest_ref, x_ref, g2_ref, y_hbm, o_ref, stage, sem):
    i = pl.program_id(0)
    n = pl.num_programs(0)
    tm = x_ref.shape[0]
    slot = i % 2

    def issue(tile, s):
        def body(r, carry):
            d = dest_ref[tile * tm + r]
            pltpu.make_async_copy(y_hbm.at[pl.ds(d, 1), :], stage.at[s, pl.ds(r, 1), :], sem.at[s]).start()
            return carry
        lax.fori_loop(0, tm, body, 0)

    @pl.when(i == 0)
    def _():
        issue(0, 0)

    @pl.when(i + 1 < n)
    def _():
        issue(i + 1, 1 - slot)

    pltpu.make_async_copy(y_hbm.at[pl.ds(0, tm), :], stage.at[slot], sem.at[slot]).wait()
    o_ref[...] = x_ref[...] + g2_ref[0] * stage[slot]


def _combine_call(stream, dest, x, g2, y):
    n = stream.rows
    row = lambda i, d: (i, 0)
    grid_spec = pltpu.PrefetchScalarGridSpec(
        num_scalar_prefetch=1,
        grid=(n // TM,),
        in_specs=[
            pl.BlockSpec((TM, D_MODEL), row),
            _mod_spec(stream, TM),
            pl.BlockSpec(memory_space=pl.ANY),
        ],
        out_specs=pl.BlockSpec((TM, D_MODEL), row),
        scratch_shapes=[pltpu.VMEM((2, TM, D_MODEL), F32), pltpu.SemaphoreType.DMA((2,))],
    )
    return pl.pallas_call(
        _combine_kernel,
        grid_spec=grid_spec,
        out_shape=jax.ShapeDtypeStruct((n, D_MODEL), F32),
        compiler_params=_cparams("arbitrary"),
        name="combine",
    )(dest, x, g2, y)


def _bucket_tables(counts, n_tiles_max):
    ntile = (counts + TMOE - 1) // TMOE
    tile_end = jnp.cumsum(ntile)
    offs = (tile_end - ntile) * TMOE
    n_used = tile_end[-1]
    t = jnp.arange(n_tiles_max, dtype=I32)
    tidx = jnp.minimum(t, n_used - 1)
    tb = jnp.sum((tile_end[None, :] <= tidx[:, None]).astype(I32), axis=1)
    tb = jnp.minimum(tb, N_BUCKETS - 1)
    grp, pair = tb // N_PAIRS, tb % N_PAIRS
    ea = grp * EXPERTS_PER_GROUP + jnp.asarray(PAIR_A, I32)[pair]
    eb = grp * EXPERTS_PER_GROUP + jnp.asarray(PAIR_B, I32)[pair]
    return offs, ea, eb, tidx, n_used.reshape(1).astype(I32)


def _moe_block(streams, xs_in, a_list, wo, mods, gain2, wr2, rb, wg, wu, wd):
    total = sum(s.rows for s in streams)
    n_tiles_max = total // TMOE + N_BUCKETS
    cnt = jnp.zeros((BUCKET_ROWS, LANE), F32)
    routed = []
    for s, x, a in zip(streams, xs_in, a_list):
        xn, ri, rw, cnt = _proj_route_call(s, a, wo, x, mods["g1"], gain2, mods["sh2"], mods["sc2"], wr2, rb, cnt)
        routed.append((xn, ri, rw))
    counts = cnt[:N_BUCKETS, 0].astype(I32)
    offs, ea, eb, tidx, nused = _bucket_tables(counts, n_tiles_max)
    sorted_buf = jnp.zeros((n_tiles_max * TMOE, ROW_W), F32)
    dests = []
    for s, (xn, ri, rw) in zip(streams, routed):
        bucket = ri[:, 0, :].reshape(s.rows)
        rank = ri[:, 1, :].reshape(s.rows)
        dest = offs[bucket] + rank
        wcols = jnp.zeros((s.rows, LANE), F32)
        wcols = wcols.at[:, 0].set(rw[:, 0, :].reshape(s.rows)).at[:, 1].set(rw[:, 1, :].reshape(s.rows))
        sorted_buf = _dispatch_call(s, dest, xn, gain2, mods["sh2"], mods["sc2"], wcols, sorted_buf)
        dests.append(dest)
    y = _moe_call(ea, eb, tidx, nused, sorted_buf, wg, wu, wd)
    return [_combine_call(s, dest, xn, mods["g2"], y) for s, dest, (xn, _, _) in zip(streams, dests, routed)]


def _gla_in_kernel(x_ref, gain_ref, sh_ref, sc_ref, w_ref, wl_ref, q_ref, k_ref, v_ref, r_ref, lr_ref):
    h = _modulate(x_ref[...], gain_ref[...], sh_ref[0], sc_ref[0]).astype(BF16)
    q_ref[...] = (_dot(h, w_ref[:, :GLA_HK]) * (GLA_DK ** -0.5)).astype(BF16)
    k_ref[...] = _dot(h, w_ref[:, GLA_HK:2 * GLA_HK]).astype(BF16)
    v_ref[...] = _dot(h, w_ref[:, 2 * GLA_HK:2 * GLA_HK + GLA_HV]).astype(BF16)
    r_ref[...] = _dot(h, w_ref[:, 2 * GLA_HK + GLA_HV:]).astype(BF16)
    lr_ref[...] = _dot(h, wl_ref[...])


def _gla_in_call(stream, x, gain, sh, sc, w_main, w_lr):
    n = stream.rows
    row = lambda i: (i, 0)
    fixed = lambda i: (0, 0)
    return pl.pallas_call(
        _gla_in_kernel,
        grid=(n // TM,),
        in_specs=[
            pl.BlockSpec((TM, D_MODEL), row),
            pl.BlockSpec((1, D_MODEL), fixed),
            _mod_spec(stream, TM), _mod_spec(stream, TM),
            _resident(w_main.shape, fixed),
            _resident(w_lr.shape, fixed),
        ],
        out_specs=[pl.BlockSpec((TM, GLA_HK), row), pl.BlockSpec((TM, GLA_HK), row),
                   pl.BlockSpec((TM, GLA_HV), row), pl.BlockSpec((TM, GLA_HV), row),
                   pl.BlockSpec((TM, LANE), row)],
        out_shape=[jax.ShapeDtypeStruct((n, GLA_HK), BF16), jax.ShapeDtypeStruct((n, GLA_HK), BF16),
                   jax.ShapeDtypeStruct((n, GLA_HV), BF16), jax.ShapeDtypeStruct((n, GLA_HV), BF16),
                   jax.ShapeDtypeStruct((n, LANE), F32)],
        compiler_params=_cparams("parallel"),
        name="gla_in",
    )(x, gain, sh, sc, w_main, w_lr)


def _gla_kernel(q_ref, k_ref, v_ref, r_ref, lr_ref, wup_ref, bg_ref, og_ref, *rest, seq, has_s0, want_state):
    rest = list(rest)
    s0_ref = rest.pop(0) if has_s0 else None
    o_ref = rest.pop(0)
    sf_ref = rest.pop(0) if want_state else None
    la_sc, oacc, st = rest
    c = GLA_CHUNK
    nc = seq // c

    lr_hi, lr_lo = _split_bf16(lr_ref[0])
    w_hi, w_lo = _split_bf16(wup_ref[0])
    z = _dot(lr_hi, w_hi) + _dot(lr_hi, w_lo) + _dot(lr_lo, w_hi) + bg_ref[0]
    la_sc[...] = (jnp.minimum(z, 0.0) - jnp.log1p(jnp.exp(-jnp.abs(z)))) * (1.0 / GLA_GATE_NORMALIZER)

    ri = lax.broadcasted_iota(I32, (c, c), 0)
    ci = lax.broadcasted_iota(I32, (c, c), 1)

    for d in range(2):
        mask = (ci <= ri) if d == 0 else (ci >= ri)
        tri = mask.astype(BF16)
        if has_s0:
            st[...] = s0_ref[0, d, 0]
        else:
            st[...] = jnp.zeros(st.shape, F32)

        def chunk(step, carry, d=d, mask=mask, tri=tri):
            n = step if d == 0 else nc - 1 - step
            rows = pl.ds(pl.multiple_of(n * c, c), c)
            la_hi, la_lo = _split_bf16(la_sc[rows, d * GLA_DK:(d + 1) * GLA_DK])
            b = _dot(tri, la_hi) + _dot(tri, la_lo)
            b_end = b[c - 1:c, :] if d == 0 else b[0:1, :]
            qf = q_ref[0, rows, :].astype(F32)
            kf = k_ref[0, rows, :].astype(F32)
            vc = v_ref[0, rows, :]
            q_dec = (qf * jnp.exp(b)).astype(BF16)
            k_inv = (kf * jnp.exp(-b)).astype(BF16)
            k_end = kf * jnp.exp(b_end - b)
            att = jnp.where(mask, _dot_nt(q_dec, k_inv), 0.0)
            o = _dot(att.astype(BF16), vc) + _dot(q_dec, st[...].astype(BF16))
            decay = jnp.broadcast_to(jnp.exp(b_end), (8, GLA_DK)).T[:, 0:1]
            st[...] = st[...] * decay + _dot(k_end.T.astype(BF16), vc)
            if d == 0:
                oacc[rows, :] = o
            else:
                o = oacc[rows, :] + o
                gate = r_ref[0, rows, :].astype(F32)
                o_ref[0, rows, :] = (_rms(o, og_ref[...]) * (gate * jax.nn.sigmoid(gate))).astype(BF16)
            return carry

        lax.fori_loop(0, nc, chunk, 0)
        if want_state:
            sf_ref[0, d, 0] = st[...]


def _gla_call(stream, q, k, v, r, lr, wup, bg, og, s0, want_state):
    b, t = stream.batch, stream.seq
    shp = lambda a: a.reshape(b, t, a.shape[-1])
    q, k, v, r, lr = shp(q), shp(k), shp(v), shp(r), shp(lr)
    per_head = lambda bi, h: (bi, 0, h)
    in_specs = [
        pl.BlockSpec((1, t, GLA_DK), per_head), pl.BlockSpec((1, t, GLA_DK), per_head),
        pl.BlockSpec((1, t, GLA_DV), per_head), pl.BlockSpec((1, t, GLA_DV), per_head),
        pl.BlockSpec((1, t, LANE), lambda bi, h: (bi, 0, 0)),
        pl.BlockSpec((1, LANE, 2 * GLA_DK), lambda bi, h: (h, 0, 0)),
        pl.BlockSpec((1, 1, 2 * GLA_DK), lambda bi, h: (h, 0, 0)),
        pl.BlockSpec((1, GLA_DV), lambda bi, h: (0, 0)),
    ]
    args = [q, k, v, r, lr, wup, bg, og]
    state_blk = pl.BlockSpec((1, 2, 1, GLA_DK, GLA_DV), lambda bi, h: (bi, 0, h, 0, 0))
    if s0 is not None:
        in_specs.append(state_blk)
        args.append(s0)
    out_specs = [pl.BlockSpec((1, t, GLA_DV), per_head)]
    out_shape = [jax.ShapeDtypeStruct((b, t, GLA_HV), BF16)]
    if want_state:
        out_specs.append(state_blk)
        out_shape.append(jax.ShapeDtypeStruct((b, 2, GLA_HEADS, GLA_DK, GLA_DV), F32))
    outs = pl.pallas_call(
        functools.partial(_gla_kernel, seq=t, has_s0=s0 is not None, want_state=want_state),
        grid=(b, GLA_HEADS),
        in_specs=in_specs,
        out_specs=out_specs,
        out_shape=out_shape,
        scratch_shapes=[pltpu.VMEM((t, 2 * GLA_DK), F32), pltpu.VMEM((t, GLA_DV), F32),
                        pltpu.VMEM((GLA_DK, GLA_DV), F32)],
        compiler_params=_cparams("parallel", "parallel"),
        name="gla",
    )(*args)
    o = outs[0].reshape(b * t, GLA_HV)
    return (o, outs[1]) if want_state else (o, None)


def _gla_gate_tables(w_gate_up, b_gate):
    wup = jnp.zeros((GLA_HEADS, LANE, 2 * GLA_DK), F32)
    bg = jnp.zeros((GLA_HEADS, 1, 2 * GLA_DK), F32)
    for d in range(2):
        w = w_gate_up[d].reshape(GLA_GATE_RANK, GLA_HEADS, GLA_DK).transpose(1, 0, 2)
        wup = wup.at[:, d * GLA_GATE_RANK:(d + 1) * GLA_GATE_RANK, d * GLA_DK:(d + 1) * GLA_DK].set(w)
        bg = bg.at[:, 0, d * GLA_DK:(d + 1) * GLA_DK].set(b_gate[d].reshape(GLA_HEADS, GLA_DK))
    return wup, bg


def kernel(x_prompt, x_sample, cache_k, cache_v, state_gla, c, c_ctx, w_ada, b_ada, norm_mix, norm_ffn,
           attn_w_qkv, attn_q_norm, attn_k_norm, attn_w_o, gla_w_in, gla_w_gate_up, gla_b_gate, gla_o_norm,
           gla_w_o, moe_w_router, moe_router_bias, moe_w_gate, moe_w_up, moe_w_down):
    batch, seq, d = x_prompt.shape
    dec_batch, dec_seq, _ = x_sample.shape
    depth = w_ada.shape[0]
    sp = _Stream(batch, seq, 0)
    ss = _Stream(dec_batch, dec_seq, 1)
    streams = (sp, ss)

    cond8 = jnp.zeros((8, d), F32).at[0].set(c_ctx).at[1:1 + dec_batch].set(c)
    ada = _ada_call(cond8, w_ada, b_ada).reshape(depth, 8, 6, 1, d)

    wr_hi = moe_w_router.astype(BF16)
    wr_lo = (moe_w_router - wr_hi.astype(F32)).astype(BF16)
    wr2 = jnp.zeros((d, LANE), BF16).at[:, :N_EXPERTS].set(wr_hi).at[:, N_EXPERTS:2 * N_EXPERTS].set(wr_lo)
    rb = jnp.broadcast_to(moe_router_bias.astype(F32)[:, None], (N_EXPERTS, TM))

    xs = [x_prompt.reshape(sp.rows, d), x_sample.reshape(ss.rows, d)]
    new_k = new_v = new_state = None
    for i in range(depth):
        names = ("sh1", "sc1", "g1", "sh2", "sc2", "g2")
        mods = {nm: ada[i, :, j] for j, nm in enumerate(names)}
        gain1 = norm_mix[i].reshape(1, d)
        gain2 = norm_ffn[i].reshape(1, d)
        j = i // 2
        if i % 2 == 0:
            w_qkv = attn_w_qkv[j].astype(BF16)
            qg = attn_q_norm[j].reshape(1, HEAD_DIM)
            kg = attn_k_norm[j].reshape(1, HEAD_DIM)
            nk = N_KV_HEADS * HEAD_DIM
            qp, kp, vp = _qkv_call(sp, xs[0], gain1, mods["sh1"], mods["sc1"], w_qkv, qg, kg, None)
            new_k, new_v = kp, vp
            ap = _attn_call(qp.reshape(batch, seq, -1),
                            [(kp.reshape(batch, seq, nk), vp.reshape(batch, seq, nk))], seq)
            qs, ks, vs = _qkv_call(ss, xs[1], gain1, mods["sh1"], mods["sc1"], w_qkv, qg, kg,
                                   _rope_tables(dec_seq))
            past = cache_k.shape[2]
            a_s = _attn_call(qs.reshape(dec_batch, dec_seq, -1),
                             [(cache_k[:, j].reshape(dec_batch, past, nk), cache_v[:, j].reshape(dec_batch, past, nk)),
                              (ks.reshape(dec_batch, dec_seq, nk), vs.reshape(dec_batch, dec_seq, nk))], TM)
            a_list = [ap.reshape(sp.rows, -1), a_s.reshape(ss.rows, -1)]
            wo = attn_w_o[j].astype(BF16)
        else:
            n_main = 2 * GLA_HK + 2 * GLA_HV
            w_main = gla_w_in[j][:, :n_main].astype(BF16)
            w_lr = jnp.zeros((d, LANE), BF16).at[:, :2 * GLA_GATE_RANK].set(gla_w_in[j][:, n_main:].astype(BF16))
            wup, bg = _gla_gate_tables(gla_w_gate_up[j], gla_b_gate[j])
            og = gla_o_norm[j].reshape(1, GLA_DV)
            a_list = []
            for s, x, s0, want in ((sp, xs[0], None, True), (ss, xs[1], state_gla[:, j], False)):
                q, k, v, r, lr = _gla_in_call(s, x, gain1, mods["sh1"], mods["sc1"], w_main, w_lr)
                o, st = _gla_call(s, q, k, v, r, lr, wup, bg, og, s0, want)
                a_list.append(o)
                if want:
                    new_state = st
            wo = gla_w_o[j].astype(BF16)
        xs = _moe_block(streams, xs, a_list, wo, mods, gain2, wr2, rb,
                        moe_w_gate[i].astype(BF16), moe_w_up[i].astype(BF16), moe_w_down[i].astype(BF16))

    new_cache_k = new_k.reshape(batch, 1, seq, N_KV_HEADS, HEAD_DIM)
    new_cache_v = new_v.reshape(batch, 1, seq, N_KV_HEADS, HEAD_DIM)
    new_state_gla = new_state.reshape(batch, 1, 2, GLA_HEADS, GLA_DK, GLA_DV).astype(state_gla.dtype)
    return (xs[0].reshape(batch, seq, d), xs[1].reshape(dec_batch, dec_seq, d),
            new_cache_k, new_cache_v, new_state_gla)
```

```python
import functools

import jax
import jax.numpy as jnp
from jax import lax
from jax.experimental import pallas as pl
from jax.experimental.pallas import tpu as pltpu

F32 = jnp.float32
BF16 = jnp.bfloat16
I32 = jnp.int32

D_MODEL = 2048
EPS = 1e-6
GRID_W = 64
HEAD_DIM = 128
N_HEADS = 16
N_KV_HEADS = 4
ROPE_BASE = 10000.0
GLA_HEADS = 4
GLA_DK = 256
GLA_DV = 512
GLA_HK = GLA_HEADS * GLA_DK
GLA_HV = GLA_HEADS * GLA_DV
GLA_GATE_RANK = 16
GLA_GATE_NORMALIZER = 16.0
GLA_CHUNK = 64
GLA_SEG = 256
N_EXPERTS = 16
N_GROUPS = 4
EXPERTS_PER_GROUP = 4
D_EXPERT = 512

PAIR_A = (0, 0, 1, 1, 0, 2)
PAIR_B = (1, 2, 2, 3, 3, 3)
N_PAIRS = len(PAIR_A)
N_BUCKETS = N_GROUPS * N_PAIRS
BUCKET_ROWS = 32
LANE = 128
ROW_W = D_MODEL + LANE
VMEM_LIMIT = 56 * 1024 * 1024

TM = 256
TMOE = 256
ADA_TN = 1536
DMA_UNROLL = 8


def _cparams(*sem):
    return pltpu.CompilerParams(dimension_semantics=sem, vmem_limit_bytes=VMEM_LIMIT)


def _resident(block_shape, index_map):
    return pl.BlockSpec(block_shape, index_map, pipeline_mode=pl.Buffered(1))


def _dot(a, b):
    return jnp.dot(a, b, preferred_element_type=F32)


def _dot_nt(a, b):
    return lax.dot_general(a, b, (((1,), (1,)), ((), ())), preferred_element_type=F32)


def _split_bf16(x):
    hi = x.astype(BF16)
    lo = (x - hi.astype(F32)).astype(BF16)
    return hi, lo


def _modulate(x, gain, shift, scale):
    ms = jnp.mean(x * x, axis=-1, keepdims=True)
    y = (x * lax.rsqrt(ms + EPS)) * gain
    return y * (1.0 + scale) + shift


def _rms(x, gain):
    ms = jnp.mean(x * x, axis=-1, keepdims=True)
    return (x * lax.rsqrt(ms + EPS)) * gain


def _ada_kernel(c_ref, w_ref, b_ref, o_ref):
    c = c_ref[...]
    s = (c * jax.nn.sigmoid(c)).astype(BF16)
    o_ref[0] = _dot(s, w_ref[0].astype(BF16)) + b_ref[0]


def _ada_call(cond8, w_ada, b_ada):
    depth, d, n6 = w_ada.shape
    return pl.pallas_call(
        _ada_kernel,
        grid=(depth, n6 // ADA_TN),
        in_specs=[
            pl.BlockSpec((8, d), lambda l, j: (0, 0)),
            pl.BlockSpec((1, d, ADA_TN), lambda l, j: (l, 0, j)),
            pl.BlockSpec((1, 1, ADA_TN), lambda l, j: (l, 0, j)),
        ],
        out_specs=pl.BlockSpec((1, 8, ADA_TN), lambda l, j: (l, 0, j)),
        out_shape=jax.ShapeDtypeStruct((depth, 8, n6), F32),
        compiler_params=_cparams("parallel", "parallel"),
        name="ada",
    )(cond8, w_ada, b_ada.reshape(depth, 1, n6))


class _Stream:
    def __init__(self, batch, seq, cond_base):
        self.batch, self.seq, self.rows, self.cond_base = batch, seq, batch * seq, cond_base

    def cond_map(self, tm):
        tiles_per_seq = self.seq // tm
        if self.cond_base == 0:
            return lambda i, *_: (0, 0, 0)
        return lambda i, *_: (self.cond_base + i // tiles_per_seq, 0, 0)


def _mod_spec(stream, tm):
    return pl.BlockSpec((1, 1, D_MODEL), stream.cond_map(tm))


def _rope(xh, cos, sin):
    lane = lax.broadcasted_iota(I32, xh.shape, 1)
    nxt = pltpu.roll(xh, HEAD_DIM - 1, 1)
    prv = pltpu.roll(xh, 1, 1)
    return xh * cos + jnp.where(lane % 2 == 0, nxt, prv) * sin


def _qkv_kernel(x_ref, gain_ref, sh_ref, sc_ref, w_ref, qg_ref, kg_ref, *rest, rope):
    if rope:
        cos_ref, sin_ref, q_ref, k_ref, v_ref = rest
        cos, sin = cos_ref[...], sin_ref[...]
    else:
        q_ref, k_ref, v_ref = rest
    h = _modulate(x_ref[...], gain_ref[...], sh_ref[0], sc_ref[0]).astype(BF16)
    nq, nk = N_HEADS * HEAD_DIM, N_KV_HEADS * HEAD_DIM
    q = _dot(h, w_ref[:, :nq])
    for hh in range(N_HEADS):
        xh = _rms(q[:, hh * HEAD_DIM:(hh + 1) * HEAD_DIM], qg_ref[...])
        if rope:
            xh = _rope(xh, cos, sin)
        q_ref[:, hh * HEAD_DIM:(hh + 1) * HEAD_DIM] = xh.astype(BF16)
    kv = _dot(h, w_ref[:, nq:])
    for hh in range(N_KV_HEADS):
        xh = _rms(kv[:, hh * HEAD_DIM:(hh + 1) * HEAD_DIM], kg_ref[...])
        if rope:
            xh = _rope(xh, cos, sin)
        k_ref[:, hh * HEAD_DIM:(hh + 1) * HEAD_DIM] = xh
    v_ref[...] = kv[:, nk:]


def _qkv_call(stream, x, gain, sh, sc, w, qg, kg, rope_tabs):
    n = stream.rows
    nqkv = w.shape[1]
    nk = N_KV_HEADS * HEAD_DIM
    row = lambda i: (i, 0)
    fixed = lambda i: (0, 0)
    in_specs = [
        pl.BlockSpec((TM, D_MODEL), row),
        pl.BlockSpec((1, D_MODEL), fixed),
        _mod_spec(stream, TM), _mod_spec(stream, TM),
        _resident((D_MODEL, nqkv), fixed),
        pl.BlockSpec((1, HEAD_DIM), fixed),
        pl.BlockSpec((1, HEAD_DIM), fixed),
    ]
    args = [x, gain, sh, sc, w, qg, kg]
    if rope_tabs is not None:
        tiles_per_seq = stream.seq // TM
        pos = lambda i: (i % tiles_per_seq, 0)
        in_specs += [pl.BlockSpec((TM, HEAD_DIM), pos), pl.BlockSpec((TM, HEAD_DIM), pos)]
        args += list(rope_tabs)
    return pl.pallas_call(
        functools.partial(_qkv_kernel, rope=rope_tabs is not None),
        grid=(n // TM,),
        in_specs=in_specs,
        out_specs=[pl.BlockSpec((TM, N_HEADS * HEAD_DIM), row),
                   pl.BlockSpec((TM, nk), row), pl.BlockSpec((TM, nk), row)],
        out_shape=[jax.ShapeDtypeStruct((n, N_HEADS * HEAD_DIM), BF16),
                   jax.ShapeDtypeStruct((n, nk), F32), jax.ShapeDtypeStruct((n, nk), F32)],
        compiler_params=_cparams("parallel"),
        name="qkv_rope" if rope_tabs is not None else "qkv",
    )(*args)


def _rope_tables(seq):
    t = jnp.arange(seq)
    rowp = (t // GRID_W).astype(F32)
    colp = (t % GRID_W).astype(F32)
    n_pairs_axis = HEAD_DIM // 4
    inv = ROPE_BASE ** (-jnp.arange(n_pairs_axis, dtype=F32) / n_pairs_axis)
    ang = jnp.concatenate([rowp[:, None] * inv, colp[:, None] * inv], axis=-1)
    cos = jnp.repeat(jnp.cos(ang), 2, axis=-1)
    sin = jnp.sin(ang)
    sin = jnp.stack([-sin, sin], axis=-1).reshape(seq, HEAD_DIM)
    return cos, sin


def _attn_kernel(q_ref, *refs, nseg, scale):
    o_ref = refs[2 * nseg]
    group = N_HEADS // N_KV_HEADS
    for g in range(N_KV_HEADS):
        sl = slice(g * HEAD_DIM, (g + 1) * HEAD_DIM)
        ks = [refs[2 * s][0, :, sl].astype(BF16) for s in range(nseg)]
        vs = [refs[2 * s + 1][0, :, sl].astype(BF16) for s in range(nseg)]
        for j in range(group):
            hs = slice((g * group + j) * HEAD_DIM, (g * group + j + 1) * HEAD_DIM)
            qh = q_ref[0, :, hs]
            ss = [_dot_nt(qh, k) * scale for k in ks]
            m = ss[0].max(axis=-1, keepdims=True)
            for s in ss[1:]:
                m = jnp.maximum(m, s.max(axis=-1, keepdims=True))
            es = [jnp.exp(s - m) for s in ss]
            den = es[0].sum(axis=-1, keepdims=True)
            for e in es[1:]:
                den = den + e.sum(axis=-1, keepdims=True)
            acc = _dot(es[0].astype(BF16), vs[0])
            for e, v in zip(es[1:], vs[1:]):
                acc = acc + _dot(e.astype(BF16), v)
            o_ref[0, :, hs] = (acc / den).astype(BF16)


def _attn_call(q, kvs, tq):
    b, t, dq = q.shape
    nseg = len(kvs)
    in_specs = [pl.BlockSpec((1, tq, dq), lambda bi, i: (bi, i, 0))]
    args = [q]
    for k, v in kvs:
        for a in (k, v):
            in_specs.append(pl.BlockSpec((1,) + a.shape[1:], lambda bi, i: (bi, 0, 0)))
            args.append(a)
    return pl.pallas_call(
        functools.partial(_attn_kernel, nseg=nseg, scale=HEAD_DIM ** -0.5),
        grid=(b, t // tq),
        in_specs=in_specs,
        out_specs=pl.BlockSpec((1, tq, dq), lambda bi, i: (bi, i, 0)),
        out_shape=jax.ShapeDtypeStruct((b, t, dq), BF16),
        compiler_params=_cparams("parallel", "parallel"),
        name="attn%d" % nseg,
    )(*args)


def _route_rows(sc, bi):
    scr = [sc[e:e + 1, :] for e in range(N_EXPERTS)]
    bir = [bi[e:e + 1, :] for e in range(N_EXPERTS)]
    gs = []
    for g in range(N_GROUPS):
        a, b, c, d = bir[4 * g:4 * g + 4]
        gs.append(jnp.maximum(jnp.maximum(jnp.maximum(a + b, a + c), jnp.maximum(a + d, b + c)),
                              jnp.maximum(b + d, c + d)))
    best, gsel = gs[0], jnp.zeros_like(gs[0], dtype=I32)
    for g in range(1, N_GROUPS):
        better = gs[g] > best
        gsel = jnp.where(better, g, gsel)
        best = jnp.where(better, gs[g], best)
    v, u = [], []
    for j in range(EXPERTS_PER_GROUP):
        vj, uj = bir[j], scr[j]
        for g in range(1, N_GROUPS):
            vj = jnp.where(gsel == g, bir[4 * g + j], vj)
            uj = jnp.where(gsel == g, scr[4 * g + j], uj)
        v.append(vj)
        u.append(uj)
    sel = []
    for i in range(EXPERTS_PER_GROUP):
        cnt = jnp.zeros_like(gsel)
        for j in range(EXPERTS_PER_GROUP):
            if j == i:
                continue
            beats = (v[j] > v[i]) | ((v[j] == v[i]) & (j < i)) if j < i else (v[j] > v[i])
            cnt = cnt + beats.astype(I32)
        sel.append(cnt < 2)
    wsum = jnp.zeros_like(u[0])
    for i in range(EXPERTS_PER_GROUP):
        wsum = wsum + jnp.where(sel[i], u[i], 0.0)
    gate = [u[i] / wsum for i in range(EXPERTS_PER_GROUP)]
    code = sum(jnp.where(sel[i], 1 << i, 0) for i in range(EXPERTS_PER_GROUP))
    pair = jnp.zeros_like(gsel)
    for p in range(N_PAIRS):
        pair = jnp.where(code == (1 << PAIR_A[p]) + (1 << PAIR_B[p]), p, pair)
    w_lo = jnp.where(sel[0], gate[0], jnp.where(sel[1], gate[1], gate[2]))
    w_hi = jnp.where(sel[3], gate[3], jnp.where(sel[2], gate[2], gate[1]))
    return gsel * N_PAIRS + pair, w_lo, w_hi


def _proj_route_kernel(a_ref, wo_ref, x_ref, g1_ref, gain_ref, sh_ref, sc_ref, wr_ref, rb_ref, cin_ref,
                       xn_ref, ri_ref, rw_ref, cout_ref, cnt_sc):
    i = pl.program_id(0)
    tm = x_ref.shape[0]

    @pl.when(i == 0)
    def _():
        cnt_sc[...] = cin_ref[...]

    xn = x_ref[...] + g1_ref[0] * _dot(a_ref[...], wo_ref[...])
    xn_ref[...] = xn
    h2 = _modulate(xn, gain_ref[...], sh_ref[0], sc_ref[0])
    hi, lo = _split_bf16(h2)
    p1 = _dot(hi, wr_ref[...])
    p2 = _dot(lo, wr_ref[...])
    logits = (p1 + pltpu.roll(p1, LANE - N_EXPERTS, 1) + p2).T[:N_EXPERTS]
    score = jax.nn.sigmoid(logits)
    bucket, w_lo, w_hi = _route_rows(score, score + rb_ref[...])

    onehot = lax.broadcasted_iota(I32, (BUCKET_ROWS, tm), 0) == bucket
    tri = (lax.broadcasted_iota(I32, (tm, tm), 0) <= lax.broadcasted_iota(I32, (tm, tm), 1))
    cum = _dot(onehot.astype(BF16), tri.astype(BF16))
    base = cnt_sc[:, 0:1]
    rank = jnp.sum(jnp.where(onehot, cum - 1.0 + base, 0.0), axis=0, keepdims=True)
    cnt_new = cnt_sc[...] + cum[:, tm - 1:tm]
    cnt_sc[...] = cnt_new
    cout_ref[...] = cnt_new

    ri_ref[0] = jnp.zeros(ri_ref.shape[1:], I32)
    ri_ref[0, 0:1, :] = bucket
    ri_ref[0, 1:2, :] = rank.astype(I32)
    rw_ref[0] = jnp.zeros(rw_ref.shape[1:], F32)
    rw_ref[0, 0:1, :] = w_lo
    rw_ref[0, 1:2, :] = w_hi


def _proj_route_call(stream, a, wo, x, g1, gain2, sh2, sc2, wr2, rb, cnt_in):
    n = stream.rows
    nt = n // TM
    row = lambda i: (i, 0)
    fixed = lambda i: (0, 0)
    tile3 = lambda i: (i, 0, 0)
    return pl.pallas_call(
        _proj_route_kernel,
        grid=(nt,),
        in_specs=[
            pl.BlockSpec((TM, a.shape[1]), row),
            _resident(wo.shape, fixed),
            pl.BlockSpec((TM, D_MODEL), row),
            _mod_spec(stream, TM),
            pl.BlockSpec((1, D_MODEL), fixed),
            _mod_spec(stream, TM), _mod_spec(stream, TM),
            _resident(wr2.shape, fixed),
            pl.BlockSpec(rb.shape, fixed),
            pl.BlockSpec(cnt_in.shape, fixed),
        ],
        out_specs=[
            pl.BlockSpec((TM, D_MODEL), row),
            pl.BlockSpec((1, 8, TM), tile3),
            pl.BlockSpec((1, 8, TM), tile3),
            pl.BlockSpec(cnt_in.shape, fixed),
        ],
        out_shape=[
            jax.ShapeDtypeStruct((n, D_MODEL), F32),
            jax.ShapeDtypeStruct((nt, 8, TM), I32),
            jax.ShapeDtypeStruct((nt, 8, TM), F32),
            jax.ShapeDtypeStruct(cnt_in.shape, F32),
        ],
        scratch_shapes=[pltpu.VMEM(cnt_in.shape, F32)],
        compiler_params=_cparams("arbitrary"),
        name="proj_route",
    )(a, wo, x, g1, gain2, sh2, sc2, wr2, rb, cnt_in)


def _row_copy_out(stage, slot, r, dst_hbm, d, sem):
    return pltpu.make_async_copy(stage.at[slot, pl.ds(r, 1), :], dst_hbm.at[pl.ds(d, 1), :], sem.at[slot])


def _dispatch_kernel(dest_ref, xn_ref, gain_ref, sh_ref, sc_ref, wc_ref, sorted_in, sorted_out, stage, sem):
    del sorted_in
    i = pl.program_id(0)
    n = pl.num_programs(0)
    tm = xn_ref.shape[0]
    slot = i % 2

    def wait_slot(s):
        pltpu.make_async_copy(stage.at[s], sorted_out.at[pl.ds(0, tm), :], sem.at[s]).wait()

    @pl.when(i >= 2)
    def _():
        wait_slot(slot)

    stage[slot, :, :D_MODEL] = _modulate(xn_ref[...], gain_ref[...], sh_ref[0], sc_ref[0])
    stage[slot, :, D_MODEL:] = wc_ref[...]

    def issue(rb, carry):
        for u in range(DMA_UNROLL):
            r = rb * DMA_UNROLL + u
            _row_copy_out(stage, slot, r, sorted_out, dest_ref[i * tm + r], sem).start()
        return carry

    lax.fori_loop(0, tm // DMA_UNROLL, issue, 0)

    @pl.when(i == n - 1)
    def _():
        wait_slot(slot)

        @pl.when(n >= 2)
        def _():
            wait_slot(1 - slot)


def _dispatch_call(stream, dest, xn, gain2, sh2, sc2, wcols, sorted_buf):
    n = stream.rows
    row = lambda i, d: (i, 0)
    fixed = lambda i, d: (0, 0)
    grid_spec = pltpu.PrefetchScalarGridSpec(
        num_scalar_prefetch=1,
        grid=(n // TM,),
        in_specs=[
            pl.BlockSpec((TM, D_MODEL), row),
            pl.BlockSpec((1, D_MODEL), fixed),
            _mod_spec(stream, TM), _mod_spec(stream, TM),
            pl.BlockSpec((TM, LANE), row),
            pl.BlockSpec(memory_space=pl.ANY),
        ],
        out_specs=pl.BlockSpec(memory_space=pl.ANY),
        scratch_shapes=[pltpu.VMEM((2, TM, ROW_W), F32), pltpu.SemaphoreType.DMA((2,))],
    )
    return pl.pallas_call(
        _dispatch_kernel,
        grid_spec=grid_spec,
        out_shape=jax.ShapeDtypeStruct(sorted_buf.shape, F32),
        input_output_aliases={6: 0},
        compiler_params=_cparams("arbitrary"),
        name="dispatch",
    )(dest, xn, gain2, sh2, sc2, wcols, sorted_buf)


def _moe_kernel(ea_ref, eb_ref, tidx_ref, nused_ref, xs_ref, wga, wua, wda, wgb, wub, wdb, y_ref):
    del ea_ref, eb_ref, tidx_ref
    t = pl.program_id(0)

    @pl.when(t < nused_ref[0])
    def _():
        x = xs_ref[:, :D_MODEL].astype(BF16)

        def expert(wg, wu, wd):
            g = _dot(x, wg[0])
            u = _dot(x, wu[0])
            hdn = (g * jax.nn.sigmoid(g)) * u
            return _dot(hdn.astype(BF16), wd[0])

        ya = expert(wga, wua, wda)
        yb = expert(wgb, wub, wdb)
        y_ref[...] = xs_ref[:, D_MODEL:D_MODEL + 1] * ya + xs_ref[:, D_MODEL + 1:D_MODEL + 2] * yb

    @pl.when(t >= nused_ref[0])
    def _():
        y_ref[...] = jnp.zeros(y_ref.shape, F32)


def _moe_call(ea, eb, tidx, nused, sorted_buf, wg, wu, wd):
    rows = sorted_buf.shape[0]
    xmap = lambda t, ea, eb, ti, nu: (ti[t], 0)
    amap = lambda t, ea, eb, ti, nu: (ea[t], 0, 0)
    bmap = lambda t, ea, eb, ti, nu: (eb[t], 0, 0)
    up_blk = (1, D_MODEL, D_EXPERT)
    dn_blk = (1, D_EXPERT, D_MODEL)
    grid_spec = pltpu.PrefetchScalarGridSpec(
        num_scalar_prefetch=4,
        grid=(rows // TMOE,),
        in_specs=[
            pl.BlockSpec((TMOE, ROW_W), xmap),
            pl.BlockSpec(up_blk, amap), pl.BlockSpec(up_blk, amap), pl.BlockSpec(dn_blk, amap),
            pl.BlockSpec(up_blk, bmap), pl.BlockSpec(up_blk, bmap), pl.BlockSpec(dn_blk, bmap),
        ],
        out_specs=pl.BlockSpec((TMOE, D_MODEL), lambda t, ea, eb, ti, nu: (t, 0)),
    )
    return pl.pallas_call(
        _moe_kernel,
        grid_spec=grid_spec,
        out_shape=jax.ShapeDtypeStruct((rows, D_MODEL), F32),
        compiler_params=_cparams("arbitrary"),
        name="moe",
    )(ea, eb, tidx, nused, sorted_buf, wg, wu, wd, wg, wu, wd)


def _combine_kernel(dest_ref, x_ref, g2_ref, y_hbm, o_ref, stage, sem):
    i = pl.program_id(0)
    n = pl.num_programs(0)
    tm = x_ref.shape[0]
    slot = i % 2

    def issue(tile, s):
        def body(rb, carry):
            for u in range(DMA_UNROLL):
                r = rb * DMA_UNROLL + u
                d = dest_ref[tile * tm + r]
                pltpu.make_async_copy(y_hbm.at[pl.ds(d, 1), :], stage.at[s, pl.ds(r, 1), :], sem.at[s]).start()
            return carry
        lax.fori_loop(0, tm // DMA_UNROLL, body, 0)

    @pl.when(i == 0)
    def _():
        issue(0, 0)

    @pl.when(i + 1 < n)
    def _():
        issue(i + 1, 1 - slot)

    pltpu.make_async_copy(y_hbm.at[pl.ds(0, tm), :], stage.at[slot], sem.at[slot]).wait()
    o_ref[...] = x_ref[...] + g2_ref[0] * stage[slot]


def _combine_call(stream, dest, x, g2, y):
    n = stream.rows
    row = lambda i, d: (i, 0)
    grid_spec = pltpu.PrefetchScalarGridSpec(
        num_scalar_prefetch=1,
        grid=(n // TM,),
        in_specs=[
            pl.BlockSpec((TM, D_MODEL), row),
            _mod_spec(stream, TM),
            pl.BlockSpec(memory_space=pl.ANY),
        ],
        out_specs=pl.BlockSpec((TM, D_MODEL), row),
        scratch_shapes=[pltpu.VMEM((2, TM, D_MODEL), F32), pltpu.SemaphoreType.DMA((2,))],
    )
    return pl.pallas_call(
        _combine_kernel,
        grid_spec=grid_spec,
        out_shape=jax.ShapeDtypeStruct((n, D_MODEL), F32),
        compiler_params=_cparams("arbitrary"),
        name="combine",
    )(dest, x, g2, y)


def _bucket_tables(counts, n_tiles_max):
    ntile = (counts + TMOE - 1) // TMOE
    tile_end = jnp.cumsum(ntile)
    offs = (tile_end - ntile) * TMOE
    n_used = tile_end[-1]
    t = jnp.arange(n_tiles_max, dtype=I32)
    tidx = jnp.minimum(t, n_used - 1)
    tb = jnp.sum((tile_end[None, :] <= tidx[:, None]).astype(I32), axis=1)
    tb = jnp.minimum(tb, N_BUCKETS - 1)
    grp, pair = tb // N_PAIRS, tb % N_PAIRS
    ea = grp * EXPERTS_PER_GROUP + jnp.asarray(PAIR_A, I32)[pair]
    eb = grp * EXPERTS_PER_GROUP + jnp.asarray(PAIR_B, I32)[pair]
    return offs, ea, eb, tidx, n_used.reshape(1).astype(I32)


def _moe_block(streams, xs_in, a_list, wo, mods, gain2, wr2, rb, wg, wu, wd, sorted_buf):
    total = sum(s.rows for s in streams)
    n_tiles_max = total // TMOE + N_BUCKETS
    cnt = jnp.zeros((BUCKET_ROWS, LANE), F32)
    routed = []
    for s, x, a in zip(streams, xs_in, a_list):
        xn, ri, rw, cnt = _proj_route_call(s, a, wo, x, mods["g1"], gain2, mods["sh2"], mods["sc2"], wr2, rb, cnt)
        routed.append((xn, ri, rw))
    counts = cnt[:N_BUCKETS, 0].astype(I32)
    offs, ea, eb, tidx, nused = _bucket_tables(counts, n_tiles_max)
    if sorted_buf is None:
        sorted_buf = jnp.zeros((n_tiles_max * TMOE, ROW_W), F32)
    dests = []
    for s, (xn, ri, rw) in zip(streams, routed):
        bucket = ri[:, 0, :].reshape(s.rows)
        rank = ri[:, 1, :].reshape(s.rows)
        dest = offs[bucket] + rank
        wcols = jnp.zeros((s.rows, LANE), F32)
        wcols = wcols.at[:, 0].set(rw[:, 0, :].reshape(s.rows)).at[:, 1].set(rw[:, 1, :].reshape(s.rows))
        sorted_buf = _dispatch_call(s, dest, xn, gain2, mods["sh2"], mods["sc2"], wcols, sorted_buf)
        dests.append(dest)
    y = _moe_call(ea, eb, tidx, nused, sorted_buf, wg, wu, wd)
    xs_out = [_combine_call(s, dest, xn, mods["g2"], y) for s, dest, (xn, _, _) in zip(streams, dests, routed)]
    return xs_out, sorted_buf


def _gla_in_kernel(x_ref, gain_ref, sh_ref, sc_ref, w_ref, wl_ref, q_ref, k_ref, v_ref, r_ref, lr_ref):
    h = _modulate(x_ref[...], gain_ref[...], sh_ref[0], sc_ref[0]).astype(BF16)
    q_ref[...] = (_dot(h, w_ref[:, :GLA_HK]) * (GLA_DK ** -0.5)).astype(BF16)
    k_ref[...] = _dot(h, w_ref[:, GLA_HK:2 * GLA_HK]).astype(BF16)
    v_ref[...] = _dot(h, w_ref[:, 2 * GLA_HK:2 * GLA_HK + GLA_HV]).astype(BF16)
    r_ref[...] = _dot(h, w_ref[:, 2 * GLA_HK + GLA_HV:]).astype(BF16)
    lr_ref[...] = _dot(h, wl_ref[...])


def _gla_in_call(stream, x, gain, sh, sc, w_main, w_lr):
    n = stream.rows
    row = lambda i: (i, 0)
    fixed = lambda i: (0, 0)
    return pl.pallas_call(
        _gla_in_kernel,
        grid=(n // TM,),
        in_specs=[
            pl.BlockSpec((TM, D_MODEL), row),
            pl.BlockSpec((1, D_MODEL), fixed),
            _mod_spec(stream, TM), _mod_spec(stream, TM),
            _resident(w_main.shape, fixed),
            _resident(w_lr.shape, fixed),
        ],
        out_specs=[pl.BlockSpec((TM, GLA_HK), row), pl.BlockSpec((TM, GLA_HK), row),
                   pl.BlockSpec((TM, GLA_HV), row), pl.BlockSpec((TM, GLA_HV), row),
                   pl.BlockSpec((TM, LANE), row)],
        out_shape=[jax.ShapeDtypeStruct((n, GLA_HK), BF16), jax.ShapeDtypeStruct((n, GLA_HK), BF16),
                   jax.ShapeDtypeStruct((n, GLA_HV), BF16), jax.ShapeDtypeStruct((n, GLA_HV), BF16),
                   jax.ShapeDtypeStruct((n, LANE), F32)],
        compiler_params=_cparams("parallel"),
        name="gla_in",
    )(x, gain, sh, sc, w_main, w_lr)


def _gla_kernel(q_ref, k_ref, v_ref, r_ref, lr_ref, wup_ref, bg_ref, og_ref, *rest, seq, has_s0, want_state):
    rest = list(rest)
    s0_ref = rest.pop(0) if has_s0 else None
    o_ref = rest.pop(0)
    sf_ref = rest.pop(0) if want_state else None
    la_sc, oacc, st = rest
    c = GLA_CHUNK
    seg = GLA_SEG
    nb = seg // c
    nseg = seq // seg
    carry_state = has_s0 or nseg > 1

    lr_hi, lr_lo = _split_bf16(lr_ref[0])
    w_hi, w_lo = _split_bf16(wup_ref[0])
    z = _dot(lr_hi, w_hi) + _dot(lr_hi, w_lo) + _dot(lr_lo, w_hi) + bg_ref[0]
    la_sc[...] = (jnp.minimum(z, 0.0) - jnp.log1p(jnp.exp(-jnp.abs(z)))) * (1.0 / GLA_GATE_NORMALIZER)

    ri = lax.broadcasted_iota(I32, (seg, seg), 0)
    ci = lax.broadcasted_iota(I32, (seg, seg), 1)
    same_chunk = (ri // c) == (ci // c)
    blk = lambda x, i: x[i * c:(i + 1) * c]

    for d in range(2):
        order = list(range(nb)) if d == 0 else list(range(nb - 1, -1, -1))
        rank = {ch: p for p, ch in enumerate(order)}
        tri = (same_chunk & ((ci <= ri) if d == 0 else (ci >= ri))).astype(BF16)
        if carry_state:
            if has_s0:
                st[...] = s0_ref[0, d, 0]
            else:
                st[...] = jnp.zeros(st.shape, F32)

        def segment(step, carry, d=d, order=order, rank=rank, tri=tri):
            if nseg == 1:
                rows = pl.ds(0, seg)
            else:
                sidx = step if d == 0 else nseg - 1 - step
                rows = pl.ds(pl.multiple_of(sidx * seg, seg), seg)
            la_hi, la_lo = _split_bf16(la_sc[rows, d * GLA_DK:(d + 1) * GLA_DK])
            b = _dot(tri, la_hi) + _dot(tri, la_lo)
            tot = [b[i * c + c - 1:i * c + c] if d == 0 else b[i * c:i * c + 1] for i in range(nb)]
            start, acc = {}, jnp.zeros((1, GLA_DK), F32)
            for ch in order:
                start[ch] = acc
                acc = acc + tot[ch]
            seg_tot = acc
            tot_rows = jnp.concatenate([jnp.broadcast_to(t, (c, GLA_DK)) for t in tot], axis=0)
            qf = q_ref[0, rows, :].astype(F32)
            kf = k_ref[0, rows, :].astype(F32)
            vs = v_ref[0, rows, :]
            q_dec = qf * jnp.exp(b)
            k_inv = kf * jnp.exp(-b)
            k_end = kf * jnp.exp(tot_rows - b)

            strips = []
            for i in range(nb):
                p = rank[i]
                need = [j for j in range(nb) if rank[j] <= p]
                lo, hi = (min(need) // 2) * 2, -(-(max(need) + 1) // 2) * 2
                parts = []
                for j in range(lo, hi):
                    if j == i:
                        parts.append(blk(k_inv, j))
                    elif rank[j] == p - 1:
                        parts.append(blk(k_end, j))
                    elif rank[j] < p:
                        parts.append(blk(k_end, j) * jnp.exp(start[i] - start[j] - tot[j]))
                    else:
                        parts.append(jnp.zeros((c, GLA_DK), F32))
                kx = jnp.concatenate(parts, axis=0).astype(BF16)
                s = _dot_nt(blk(q_dec, i).astype(BF16), kx)
                row_t = lax.broadcasted_iota(I32, s.shape, 0) + i * c
                col_t = lax.broadcasted_iota(I32, s.shape, 1) + lo * c
                s = jnp.where((col_t <= row_t) if d == 0 else (col_t >= row_t), s, 0.0)
                pads = [jnp.zeros((c, lo * c), F32)] if lo > 0 else []
                pads.append(s)
                if hi < nb:
                    pads.append(jnp.zeros((c, (nb - hi) * c), F32))
                strips.append(jnp.concatenate(pads, axis=1) if len(pads) > 1 else s)
            att = jnp.concatenate(strips, axis=0).astype(BF16)
            o = _dot(att, vs)

            k_fin = jnp.concatenate(
                [blk(k_end, j) if j == order[-1] else blk(k_end, j) * jnp.exp(seg_tot - start[j] - tot[j])
                 for j in range(nb)], axis=0)
            upd = _dot(k_fin.T.astype(BF16), vs)
            if carry_state:
                q_seg = jnp.concatenate(
                    [blk(q_dec, i) if i == order[0] else blk(q_dec, i) * jnp.exp(start[i]) for i in range(nb)],
                    axis=0)
                o = o + _dot(q_seg.astype(BF16), st[...].astype(BF16))
                decay = jnp.broadcast_to(jnp.exp(seg_tot), (LANE, GLA_DK)).T
                st[...] = st[...] * jnp.concatenate([decay] * (GLA_DV // LANE), axis=1) + upd
            elif want_state:
                sf_ref[0, d, 0] = upd
            if d == 0:
                oacc[rows, :] = o
            else:
                o = oacc[rows, :] + o
                gate = r_ref[0, rows, :].astype(F32)
                o_ref[0, rows, :] = (_rms(o, og_ref[...]) * (gate * jax.nn.sigmoid(gate))).astype(BF16)
            return carry

        if nseg == 1:
            segment(0, 0)
        else:
            lax.fori_loop(0, nseg, segment, 0)
        if want_state and carry_state:
            sf_ref[0, d, 0] = st[...]


def _gla_call(stream, q, k, v, r, lr, wup, bg, og, s0, want_state):
    b, t = stream.batch, stream.seq
    shp = lambda a: a.reshape(b, t, a.shape[-1])
    q, k, v, r, lr = shp(q), shp(k), shp(v), shp(r), shp(lr)
    per_head = lambda bi, h: (bi, 0, h)
    in_specs = [
        pl.BlockSpec((1, t, GLA_DK), per_head), pl.BlockSpec((1, t, GLA_DK), per_head),
        pl.BlockSpec((1, t, GLA_DV), per_head), pl.BlockSpec((1, t, GLA_DV), per_head),
        pl.BlockSpec((1, t, LANE), lambda bi, h: (bi, 0, 0)),
        pl.BlockSpec((1, LANE, 2 * GLA_DK), lambda bi, h: (h, 0, 0)),
        pl.BlockSpec((1, 1, 2 * GLA_DK), lambda bi, h: (h, 0, 0)),
        pl.BlockSpec((1, GLA_DV), lambda bi, h: (0, 0)),
    ]
    args = [q, k, v, r, lr, wup, bg, og]
    state_blk = pl.BlockSpec((1, 2, 1, GLA_DK, GLA_DV), lambda bi, h: (bi, 0, h, 0, 0))
    if s0 is not None:
        in_specs.append(state_blk)
        args.append(s0)
    out_specs = [pl.BlockSpec((1, t, GLA_DV), per_head)]
    out_shape = [jax.ShapeDtypeStruct((b, t, GLA_HV), BF16)]
    if want_state:
        out_specs.append(state_blk)
        out_shape.append(jax.ShapeDtypeStruct((b, 2, GLA_HEADS, GLA_DK, GLA_DV), F32))
    outs = pl.pallas_call(
        functools.partial(_gla_kernel, seq=t, has_s0=s0 is not None, want_state=want_state),
        grid=(b, GLA_HEADS),
        in_specs=in_specs,
        out_specs=out_specs,
        out_shape=out_shape,
        scratch_shapes=[pltpu.VMEM((t, 2 * GLA_DK), F32), pltpu.VMEM((t, GLA_DV), F32),
                        pltpu.VMEM((GLA_DK, GLA_DV), F32)],
        compiler_params=_cparams("parallel", "parallel"),
        name="gla",
    )(*args)
    o = outs[0].reshape(b * t, GLA_HV)
    return (o, outs[1]) if want_state else (o, None)


def _gla_gate_tables(w_gate_up, b_gate):
    wup = jnp.zeros((GLA_HEADS, LANE, 2 * GLA_DK), F32)
    bg = jnp.zeros((GLA_HEADS, 1, 2 * GLA_DK), F32)
    for d in range(2):
        w = w_gate_up[d].reshape(GLA_GATE_RANK, GLA_HEADS, GLA_DK).transpose(1, 0, 2)
        wup = wup.at[:, d * GLA_GATE_RANK:(d + 1) * GLA_GATE_RANK, d * GLA_DK:(d + 1) * GLA_DK].set(w)
        bg = bg.at[:, 0, d * GLA_DK:(d + 1) * GLA_DK].set(b_gate[d].reshape(GLA_HEADS, GLA_DK))
    return wup, bg


def kernel(x_prompt, x_sample, cache_k, cache_v, state_gla, c, c_ctx, w_ada, b_ada, norm_mix, norm_ffn,
           attn_w_qkv, attn_q_norm, attn_k_norm, attn_w_o, gla_w_in, gla_w_gate_up, gla_b_gate, gla_o_norm,
           gla_w_o, moe_w_router, moe_router_bias, moe_w_gate, moe_w_up, moe_w_down):
    batch, seq, d = x_prompt.shape
    dec_batch, dec_seq, _ = x_sample.shape
    depth = w_ada.shape[0]
    sp = _Stream(batch, seq, 0)
    ss = _Stream(dec_batch, dec_seq, 1)
    streams = (sp, ss)

    cond8 = jnp.zeros((8, d), F32).at[0].set(c_ctx).at[1:1 + dec_batch].set(c)
    ada = _ada_call(cond8, w_ada, b_ada).reshape(depth, 8, 6, 1, d)

    wr_hi = moe_w_router.astype(BF16)
    wr_lo = (moe_w_router - wr_hi.astype(F32)).astype(BF16)
    wr2 = jnp.zeros((d, LANE), BF16).at[:, :N_EXPERTS].set(wr_hi).at[:, N_EXPERTS:2 * N_EXPERTS].set(wr_lo)
    rb = jnp.broadcast_to(moe_router_bias.astype(F32)[:, None], (N_EXPERTS, TM))

    xs = [x_prompt.reshape(sp.rows, d), x_sample.reshape(ss.rows, d)]
    new_k = new_v = new_state = sorted_buf = None
    for i in range(depth):
        names = ("sh1", "sc1", "g1", "sh2", "sc2", "g2")
        mods = {nm: ada[i, :, j] for j, nm in enumerate(names)}
        gain1 = norm_mix[i].reshape(1, d)
        gain2 = norm_ffn[i].reshape(1, d)
        j = i // 2
        if i % 2 == 0:
            w_qkv = attn_w_qkv[j].astype(BF16)
            qg = attn_q_norm[j].reshape(1, HEAD_DIM)
            kg = attn_k_norm[j].reshape(1, HEAD_DIM)
            nk = N_KV_HEADS * HEAD_DIM
            qp, kp, vp = _qkv_call(sp, xs[0], gain1, mods["sh1"], mods["sc1"], w_qkv, qg, kg, None)
            new_k, new_v = kp, vp
            ap = _attn_call(qp.reshape(batch, seq, -1),
                            [(kp.reshape(batch, seq, nk), vp.reshape(batch, seq, nk))], seq)
            qs, ks, vs = _qkv_call(ss, xs[1], gain1, mods["sh1"], mods["sc1"], w_qkv, qg, kg,
                                   _rope_tables(dec_seq))
            past = cache_k.shape[2]
            a_s = _attn_call(qs.reshape(dec_batch, dec_seq, -1),
                             [(cache_k[:, j].reshape(dec_batch, past, nk), cache_v[:, j].reshape(dec_batch, past, nk)),
                              (ks.reshape(dec_batch, dec_seq, nk), vs.reshape(dec_batch, dec_seq, nk))], TM)
            a_list = [ap.reshape(sp.rows, -1), a_s.reshape(ss.rows, -1)]
            wo = attn_w_o[j].astype(BF16)
        else:
            n_main = 2 * GLA_HK + 2 * GLA_HV
            w_main = gla_w_in[j][:, :n_main].astype(BF16)
            w_lr = jnp.zeros((d, LANE), BF16).at[:, :2 * GLA_GATE_RANK].set(gla_w_in[j][:, n_main:].astype(BF16))
            wup, bg = _gla_gate_tables(gla_w_gate_up[j], gla_b_gate[j])
            og = gla_o_norm[j].reshape(1, GLA_DV)
            a_list = []
            for s, x, s0, want in ((sp, xs[0], None, True), (ss, xs[1], state_gla[:, j], False)):
                q, k, v, r, lr = _gla_in_call(s, x, gain1, mods["sh1"], mods["sc1"], w_main, w_lr)
                o, st = _gla_call(s, q, k, v, r, lr, wup, bg, og, s0, want)
                a_list.append(o)
                if want:
                    new_state = st
            wo = gla_w_o[j].astype(BF16)
        xs, sorted_buf = _moe_block(streams, xs, a_list, wo, mods, gain2, wr2, rb, moe_w_gate[i].astype(BF16),
                                    moe_w_up[i].astype(BF16), moe_w_down[i].astype(BF16), sorted_buf)

    new_cache_k = new_k.reshape(batch, 1, seq, N_KV_HEADS, HEAD_DIM)
    new_cache_v = new_v.reshape(batch, 1, seq, N_KV_HEADS, HEAD_DIM)
    new_state_gla = new_state.reshape(batch, 1, 2, GLA_HEADS, GLA_DK, GLA_DV).astype(state_gla.dtype)
    return (xs[0].reshape(batch, seq, d), xs[1].reshape(dec_batch, dec_seq, d),
            new_cache_k, new_cache_v, new_state_gla)
```
